```python
import math
import jax, jax.numpy as jnp
from jax import lax
import numpy as np

D_MODEL = 2048
BATCH = 4
SEQ = 4096
DEPTH = 1
DEC_BATCH = 16
DEC_SEQ = 2048
PAST_LEN = 128

HEAD_DIM = 64
ATTN_WIDTH = D_MODEL // 2
ATTN_HEADS = ATTN_WIDTH // HEAD_DIM
KV_HEADS = ATTN_HEADS // 4
KV_WIDTH = KV_HEADS * HEAD_DIM
WINDOW = 128
ATTN_BLOCK = 128
REL_BUCKETS = 32
REL_MAX_DIST = 128
NEG_INF = -1e30
RWKV_WIDTH = D_MODEL - ATTN_WIDTH
RWKV_HEAD = 64
RWKV_HEADS = RWKV_WIDTH // RWKV_HEAD
DECAY_LORA = 64
ICLR_LORA = 64
GATE_LORA = 160
GN_EPS = 64e-5
RWKV_COLS = 3 * RWKV_WIDTH + GATE_LORA + 2 * DECAY_LORA + 2 * ICLR_LORA
N_IN_COLS = ATTN_WIDTH + 2 * KV_WIDTH + RWKV_COLS
N_GROUPS = 8
EXPERTS_PER_GROUP = 8
N_EXPERTS = N_GROUPS * EXPERTS_PER_GROUP
TOP_K = 2
D_EXPERT = 512
MOE_BLOCK = 128
LN_EPS = 1e-5

kernel_name = "hymba_swa_rwkv7_hmoe_encoder"


def layer_norm(x, g, b):
    xf = x.astype(jnp.float32)
    mu = jnp.mean(xf, -1, keepdims=True)
    var = jnp.mean(jnp.square(xf - mu), -1, keepdims=True)
    return ((xf - mu) * lax.rsqrt(var + LN_EPS) * g + b).astype(x.dtype)


def t5_bucket(rel):
    half = REL_BUCKETS // 2
    max_exact = half // 2
    ret = jnp.where(rel > 0, half, 0)
    n = jnp.abs(rel)
    nf = jnp.maximum(n, 1).astype(jnp.float32)
    large = max_exact + (jnp.log(nf / max_exact) / math.log(REL_MAX_DIST / max_exact)
                         * (half - max_exact)).astype(jnp.int32)
    large = jnp.minimum(large, half - 1)
    return ret + jnp.where(n < max_exact, n, large)


def windowed_sink_attention(q, k, v, sink, rel_bias):
    B, S = q.shape[0], q.shape[1]
    nb = S // ATTN_BLOCK
    G = ATTN_HEADS // KV_HEADS
    qb = q.reshape(B, nb, ATTN_BLOCK, KV_HEADS, G, HEAD_DIM)

    def band(t):
        tp = jnp.pad(t, ((0, 0), (ATTN_BLOCK, ATTN_BLOCK), (0, 0), (0, 0)))
        tp = tp.reshape(B, nb + 2, ATTN_BLOCK, KV_HEADS, HEAD_DIM)
        return jnp.concatenate([tp[:, :-2], tp[:, 1:-1], tp[:, 2:]], axis=2)

    kw, vw = band(k), band(v)
    qi = jnp.arange(ATTN_BLOCK)[:, None]
    kj = jnp.arange(3 * ATTN_BLOCK)[None, :]
    rel = kj - ATTN_BLOCK - qi
    bias = rel_bias[t5_bucket(rel)].astype(jnp.float32)
    bias = jnp.transpose(bias, (2, 0, 1)).reshape(KV_HEADS, G, ATTN_BLOCK, 3 * ATTN_BLOCK)
    kpos = jnp.arange(nb)[:, None, None] * ATTN_BLOCK - ATTN_BLOCK + kj[None]
    mask = (jnp.abs(rel) <= WINDOW)[None] & (kpos >= 0) & (kpos < S)
    scores = jnp.einsum('bnqhgd,bnkhd->bnhgqk', qb, kw,
                        preferred_element_type=jnp.float32) * (HEAD_DIM ** -0.5)
    logits = jnp.where(mask[None, :, None, None], scores + bias, NEG_INF)
    sink_l = sink.astype(jnp.float32).reshape(KV_HEADS, G, 1, 1)
    m = jnp.maximum(jnp.max(logits, -1, keepdims=True), sink_l)
    p = jnp.exp(logits - m)
    denom = jnp.sum(p, -1, keepdims=True) + jnp.exp(sink_l - m)
    probs = (p / denom).astype(v.dtype)
    out = jnp.einsum('bnhgqk,bnkhd->bnqhgd', probs, vw)
    return out.reshape(B, S, ATTN_WIDTH)


def centred_shift(p, mu):
    zeros = jnp.zeros_like(p[:, :1])
    prev = jnp.concatenate([zeros, p[:, :-1]], axis=1)
    nxt = jnp.concatenate([p[:, 1:], zeros], axis=1)
    return p + mu * (0.5 * (prev + nxt) - p)


def rwkv7_bidirectional(p, mu, w0, w2, a0, a2, g2, k_k, k_a, r_k, gn_g, gn_b):
    B, S = p.shape[0], p.shape[1]
    f32 = jnp.float32
    RW = RWKV_WIDTH
    p = centred_shift(p, mu)
    r, k, v, gd, wd, ad = jnp.split(
        p, [RW, 2 * RW, 3 * RW, 3 * RW + GATE_LORA, 3 * RW + GATE_LORA + 2 * DECAY_LORA], axis=-1)
    wd = wd.reshape(B, S, 2, DECAY_LORA)
    ad = ad.reshape(B, S, 2, ICLR_LORA)
    wlog = (w0 + jnp.einsum('bsdr,drc->bsdc', jnp.tanh(wd), w2)).astype(f32)
    decay = jnp.exp(-jnp.exp(-jax.nn.softplus(-wlog) - 0.5))
    a = jax.nn.sigmoid((a0 + jnp.einsum('bsdr,drc->bsdc', ad, a2)).astype(f32))
    g = jnp.einsum('bsr,rc->bsc', jax.nn.sigmoid(gd), g2).astype(f32)
    rf, kf, vf = r.astype(f32), k.astype(f32), v.astype(f32)
    kd = kf[:, :, None] * (1.0 + (a - 1.0) * k_a)

    def heads(t):
        return t.reshape(t.shape[:-1] + (RWKV_HEADS, RWKV_HEAD))

    kk = heads(kf * k_k)
    kk = kk / jnp.maximum(jnp.sqrt(jnp.sum(kk * kk, -1, keepdims=True)), 1e-12)
    r_h, v_h = heads(rf), heads(vf)
    kd_h, a_h, w_h = heads(kd), heads(a), heads(decay)
    dir_shape = kd_h.shape

    def to_scan(t):
        t = jnp.stack([t[:, :, 0], jnp.flip(t[:, :, 1], axis=1)], axis=0)
        return jnp.transpose(t, (2, 0, 1, 3, 4))

    def both(t):
        return jnp.broadcast_to(t[:, :, None], dir_shape)

    def step(state, inp):
        r_t, w_t, k_t, v_t, a_t, b_t = inp
        sa = jnp.einsum('dbhvk,dbhk->dbhv', state, a_t)
        state = (state * w_t[..., None, :] + sa[..., :, None] * b_t[..., None, :]
                 + v_t[..., :, None] * k_t[..., None, :])
        y = jnp.einsum('dbhvk,dbhk->dbhv', state, r_t)
        return state, y

    state0 = jnp.zeros((2, B, RWKV_HEADS, RWKV_HEAD, RWKV_HEAD), f32)
    xs = (to_scan(both(r_h)), to_scan(w_h), to_scan(kd_h), to_scan(both(v_h)),
          to_scan(both(-kk)), to_scan(both(kk)[...] * a_h))
    _, ys = lax.scan(step, state0, xs)
    ys = jnp.transpose(ys, (1, 2, 0, 3, 4))
    y = ys[0] + jnp.flip(ys[1], axis=1)
    bonus = jnp.sum(jnp.sum(r_h[:, :, None] * kd_h * r_k, -1, keepdims=True) * v_h[:, :, None], axis=2)
    y = y + bonus
    ym = jnp.mean(y, -1, keepdims=True)
    yv = jnp.mean(jnp.square(y - ym), -1, keepdims=True)
    y = ((y - ym) * lax.rsqrt(yv + GN_EPS)).reshape(B, S, RW) * gn_g + gn_b
    return (y * g).astype(p.dtype)


def hierarchical_moe(h, wr_group, br_group, wr_expert, br_expert, w_gate, w_up, w_down):
    B, S, D = h.shape
    T = B * S
    f32 = jnp.float32
    x = h.reshape(T, D)
    g_logits = (x @ wr_group).astype(f32) + br_group
    g_prob = jax.nn.softmax(g_logits, axis=-1)
    g_idx = jnp.argmax(g_logits, axis=-1).astype(jnp.int32)
    g_gate = jnp.take_along_axis(g_prob, g_idx[:, None], axis=-1)
    e_logits = ((x @ wr_expert).astype(f32) + br_expert).reshape(T, N_GROUPS, EXPERTS_PER_GROUP)
    e_logits = jnp.take_along_axis(e_logits, g_idx[:, None, None], axis=1)[:, 0]
    top_l, top_i = lax.top_k(e_logits, TOP_K)
    gates = g_gate * jax.nn.softmax(top_l, axis=-1)
    expert_id = g_idx[:, None] * EXPERTS_PER_GROUP + top_i.astype(jnp.int32)

    N = T * TOP_K
    flat_e = expert_id.reshape(N)
    flat_tok = jnp.repeat(jnp.arange(T, dtype=jnp.int32), TOP_K)
    flat_w = gates.reshape(N)
    order = jnp.argsort(flat_e)
    sorted_e = flat_e[order]
    counts = jnp.bincount(flat_e, length=N_EXPERTS).astype(jnp.int32)
    start = jnp.cumsum(counts) - counts
    padded = (counts + MOE_BLOCK - 1) // MOE_BLOCK * MOE_BLOCK
    pad_end = jnp.cumsum(padded)
    pad_start = pad_end - padded
    dest = pad_start[sorted_e] + jnp.arange(N, dtype=jnp.int32) - start[sorted_e]
    n_blocks = -(-N // MOE_BLOCK) + N_EXPERTS
    R = n_blocks * MOE_BLOCK
    row_tok = jnp.full((R,), T, jnp.int32).at[dest].set(flat_tok[order])
    row_w = jnp.zeros((R,), f32).at[dest].set(flat_w[order])
    blk_start = jnp.arange(n_blocks, dtype=jnp.int32) * MOE_BLOCK
    blk_e = jnp.minimum(jnp.searchsorted(pad_end, blk_start, side='right'), N_EXPERTS - 1)
    x_pad = jnp.concatenate([x, jnp.zeros((1, D), x.dtype)], axis=0)
    xb = x_pad[row_tok].reshape(n_blocks, MOE_BLOCK, D)

    def expert_block(args):
        xe, e = args
        hid = jax.nn.silu(xe @ w_gate[e]) * (xe @ w_up[e])
        return hid @ w_down[e]

    yb = lax.map(expert_block, (xb, blk_e)).reshape(R, D)
    y = jax.ops.segment_sum(yb * row_w[:, None].astype(yb.dtype), row_tok, num_segments=T + 1)[:T]
    return y.reshape(B, S, D).astype(h.dtype)


def encoder_layer(x, rel_bias, w_in, attn_sink, shift_mu, decay_w0, decay_w2, iclr_a0, iclr_a2,
                  gate_g2, k_k, k_a, r_k, gn_g, gn_b, w_out, ln1_g, ln1_b,
                  router_group_w, router_group_b, router_expert_w, router_expert_b,
                  w_gate, w_up, w_down, ln2_g, ln2_b):
    alpha = (2.0 * DEPTH) ** 0.25
    B, S, _ = x.shape
    proj = x @ w_in
    q, k, v, p_rwkv = jnp.split(proj, [ATTN_WIDTH, ATTN_WIDTH + KV_WIDTH, ATTN_WIDTH + 2 * KV_WIDTH], axis=-1)
    q = q.reshape(B, S, ATTN_HEADS, HEAD_DIM)
    k = k.reshape(B, S, KV_HEADS, HEAD_DIM)
    v = v.reshape(B, S, KV_HEADS, HEAD_DIM)
    attn = windowed_sink_attention(q, k, v, attn_sink, rel_bias)
    rw = rwkv7_bidirectional(p_rwkv, shift_mu, decay_w0, decay_w2, iclr_a0, iclr_a2, gate_g2,
                             k_k, k_a, r_k, gn_g, gn_b)
    mix = jnp.concatenate([attn, rw], axis=-1) @ w_out
    h = layer_norm(alpha * x + mix, ln1_g, ln1_b)
    ffn = hierarchical_moe(h, router_group_w, router_group_b, router_expert_w, router_expert_b,
                           w_gate, w_up, w_down)
    return layer_norm(alpha * h + ffn, ln2_g, ln2_b)


def encoder_trunk(x, rel_bias, w_in, attn_sink, shift_mu, decay_w0, decay_w2, iclr_a0, iclr_a2,
                  gate_g2, k_k, k_a, r_k, gn_g, gn_b, w_out, ln1_g, ln1_b,
                  router_group_w, router_group_b, router_expert_w, router_expert_b,
                  w_gate, w_up, w_down, ln2_g, ln2_b):
    for l in range(DEPTH):
        x = encoder_layer(x, rel_bias, w_in[l], attn_sink[l], shift_mu[l], decay_w0[l], decay_w2[l],
                          iclr_a0[l], iclr_a2[l], gate_g2[l], k_k[l], k_a[l], r_k[l], gn_g[l], gn_b[l],
                          w_out[l], ln1_g[l], ln1_b[l], router_group_w[l], router_group_b[l],
                          router_expert_w[l], router_expert_b[l], w_gate[l], w_up[l], w_down[l],
                          ln2_g[l], ln2_b[l])
    return x


def setup_inputs(seed: int = 0) -> dict:
    key = jax.random.key(seed)
    ks = jax.random.split(key, 32)
    f32 = jnp.float32
    beta = (8.0 * DEPTH) ** -0.25
    D, L, RW = D_MODEL, DEPTH, RWKV_WIDTH

    def nrm(k, shape, s):
        return jax.random.normal(k, shape, f32) * s

    col_scale = jnp.concatenate([
        jnp.ones((ATTN_WIDTH + KV_WIDTH,), f32), jnp.full((KV_WIDTH,), beta, f32),
        jnp.ones((2 * RW,), f32), jnp.full((RW,), beta, f32),
        jnp.ones((GATE_LORA + 2 * DECAY_LORA + 2 * ICLR_LORA,), f32)])
    return {
        "x_prompt": nrm(ks[0], (BATCH, SEQ, D), 1.0),
        "x_sample": nrm(ks[1], (DEC_BATCH, DEC_SEQ, D), 1.0),
        "rel_bias": nrm(ks[2], (REL_BUCKETS, ATTN_HEADS), 0.5),
        "w_in": nrm(ks[3], (L, D, N_IN_COLS), D ** -0.5) * col_scale,
        "attn_sink": nrm(ks[4], (L, ATTN_HEADS), 0.5),
        "shift_mu": jax.random.uniform(ks[5], (L, RWKV_COLS), f32, 0.1, 0.9),
        "decay_w0": jax.random.uniform(ks[6], (L, 2, RW), f32, -6.0, 1.0),
        "decay_w2": nrm(ks[7], (L, 2, DECAY_LORA, RW), 0.5 * DECAY_LORA ** -0.5),
        "iclr_a0": nrm(ks[8], (L, 2, RW), 0.1),
        "iclr_a2": nrm(ks[9], (L, 2, ICLR_LORA, RW), 0.5 * ICLR_LORA ** -0.5),
        "gate_g2": nrm(ks[10], (L, GATE_LORA, RW), GATE_LORA ** -0.5),
        "k_k": 0.85 + nrm(ks[11], (L, RW), 0.02),
        "k_a": 1.0 + nrm(ks[12], (L, RW), 0.02),
        "r_k": nrm(ks[13], (L, RWKV_HEADS, RWKV_HEAD), 0.1),
        "gn_g": 1.0 + nrm(ks[14], (L, RW), 0.02),
        "gn_b": nrm(ks[15], (L, RW), 0.02),
        "w_out": nrm(ks[16], (L, D, D), beta * D ** -0.5),
        "ln1_g": 1.0 + nrm(ks[17], (L, D), 0.02),
        "ln1_b": nrm(ks[18], (L, D), 0.02),
        "router_group_w": nrm(ks[19], (L, D, N_GROUPS), D ** -0.5),
        "router_group_b": nrm(ks[20], (L, N_GROUPS), 0.01),
        "router_expert_w": nrm(ks[21], (L, D, N_EXPERTS), D ** -0.5),
        "router_expert_b": nrm(ks[22], (L, N_EXPERTS), 0.01),
        "w_gate": nrm(ks[23], (L, N_EXPERTS, D, D_EXPERT), D ** -0.5),
        "w_up": nrm(ks[24], (L, N_EXPERTS, D, D_EXPERT), D ** -0.5),
        "w_down": nrm(ks[25], (L, N_EXPERTS, D_EXPERT, D), beta * D_EXPERT ** -0.5),
        "ln2_g": 1.0 + nrm(ks[26], (L, D), 0.02),
        "ln2_b": nrm(ks[27], (L, D), 0.02),
    }


def reference(x_prompt, x_sample, rel_bias, w_in, attn_sink, shift_mu, decay_w0, decay_w2,
              iclr_a0, iclr_a2, gate_g2, k_k, k_a, r_k, gn_g, gn_b, w_out, ln1_g, ln1_b,
              router_group_w, router_group_b, router_expert_w, router_expert_b,
              w_gate, w_up, w_down, ln2_g, ln2_b):
    y_prompt = encoder_trunk(x_prompt, rel_bias, w_in, attn_sink, shift_mu, decay_w0, decay_w2,
                             iclr_a0, iclr_a2, gate_g2, k_k, k_a, r_k, gn_g, gn_b, w_out, ln1_g, ln1_b,
                             router_group_w, router_group_b, router_expert_w, router_expert_b,
                             w_gate, w_up, w_down, ln2_g, ln2_b)
    y_sample = encoder_trunk(x_sample, rel_bias, w_in, attn_sink, shift_mu, decay_w0, decay_w2,
                             iclr_a0, iclr_a2, gate_g2, k_k, k_a, r_k, gn_g, gn_b, w_out, ln1_g, ln1_b,
                             router_group_w, router_group_b, router_expert_w, router_expert_b,
                             w_gate, w_up, w_down, ln2_g, ln2_b)
    return (y_prompt, y_sample)
```

```python
import functools
import math

import jax
import jax.numpy as jnp
import numpy as np
from jax import lax
from jax.experimental import pallas as pl
from jax.experimental.pallas import tpu as pltpu

F32 = jnp.float32
BF16 = jnp.bfloat16

D_MODEL = 2048
DEPTH = 1
HEAD_DIM = 64
ATTN_WIDTH = 1024
ATTN_HEADS = 16
KV_HEADS = 4
KV_WIDTH = 256
WINDOW = 128
ATTN_BLOCK = 128
REL_BUCKETS = 32
REL_MAX_DIST = 128
NEG_INF = -1e30
RWKV_WIDTH = 1024
RWKV_HEAD = 64
RWKV_HEADS = 16
DECAY_LORA = 64
ICLR_LORA = 64
GATE_LORA = 160
GN_EPS = 64e-5
N_GROUPS = 8
EXPERTS_PER_GROUP = 8
N_EXPERTS = 64
D_EXPERT = 512
LN_EPS = 1e-5
ALPHA = (2.0 * DEPTH) ** 0.25

LANES = 128
SUBLANES = 8
VMEM_LIMIT_BYTES = 56 * 1024 * 1024

COL_Q = 0
COL_KDUP = 1024
COL_VDUP = 1536
COL_R = 2048
COL_K = 3072
COL_V = 4096
COL_LORA = 5120
LORA_W = 512
GATE_PAD = 256
NP_COLS = 5632
PROJ_TN = 512
N_ATTN_STEPS = COL_R // PROJ_TN

RWKV_CHUNK = 64
PAIR = 2 * RWKV_HEAD
N_PAIRS = RWKV_HEADS // 2


def _packed_column_index():
    idx = np.full((NP_COLS,), -1, np.int64)
    idx[COL_Q:COL_Q + ATTN_WIDTH] = np.arange(ATTN_WIDTH)
    k0 = ATTN_WIDTH
    v0 = ATTN_WIDTH + KV_WIDTH
    for j in range(KV_HEADS):
        for rep in range(2):
            dst = j * PAIR + rep * HEAD_DIM
            idx[COL_KDUP + dst:COL_KDUP + dst + HEAD_DIM] = k0 + j * HEAD_DIM + np.arange(HEAD_DIM)
            idx[COL_VDUP + dst:COL_VDUP + dst + HEAD_DIM] = v0 + j * HEAD_DIM + np.arange(HEAD_DIM)
    p0 = ATTN_WIDTH + 2 * KV_WIDTH
    idx[COL_R:COL_R + 3 * RWKV_WIDTH] = p0 + np.arange(3 * RWKV_WIDTH)
    g0 = p0 + 3 * RWKV_WIDTH
    idx[COL_LORA:COL_LORA + GATE_LORA] = g0 + np.arange(GATE_LORA)
    idx[COL_LORA + GATE_PAD:COL_LORA + LORA_W] = g0 + GATE_LORA + np.arange(2 * DECAY_LORA + 2 * ICLR_LORA)
    return idx


def _dot(a, b):
    return jnp.dot(a, b, preferred_element_type=F32)


def _dot_nt(a, b):
    return lax.dot_general(a, b, (((1,), (1,)), ((), ())), preferred_element_type=F32)


def _split2(x):
    hi = x.astype(BF16)
    lo = (x - hi.astype(F32)).astype(BF16)
    return hi, lo


def _split3(x):
    hi = x.astype(BF16)
    r1 = x - hi.astype(F32)
    mid = r1.astype(BF16)
    lo = (r1 - mid.astype(F32)).astype(BF16)
    return hi, mid, lo


def _proj_kernel(x_ref, xp_ref, xn_ref, w_ref, mu_ref, o_ref, xb_ref, hb_ref, *, tiles_per_seq, tm):
    i = pl.program_id(0)
    j = pl.program_id(1)

    @pl.when(j == 0)
    def _():
        xb_ref[...] = x_ref[...].astype(BF16)
        hb_ref[0:SUBLANES, :] = xp_ref[...].astype(BF16)
        hb_ref[SUBLANES:2 * SUBLANES, :] = xn_ref[...].astype(BF16)

    acc = _dot(xb_ref[...], w_ref[...])

    @pl.when(j < N_ATTN_STEPS)
    def _():
        o_ref[...] = acc

    @pl.when(j >= N_ATTN_STEPS)
    def _():
        halo = _dot(hb_ref[...], w_ref[...])
        pos = i % tiles_per_seq
        prev_row = jnp.where(pos == 0, 0.0, halo[SUBLANES - 1:SUBLANES, :])
        next_row = jnp.where(pos == tiles_per_seq - 1, 0.0, halo[SUBLANES:SUBLANES + 1, :])
        rows = lax.broadcasted_iota(jnp.int32, (tm, 1), 0)
        prev = jnp.where(rows == 0, prev_row, pltpu.roll(acc, 1, 0))
        nxt = jnp.where(rows == tm - 1, next_row, pltpu.roll(acc, tm - 1, 0))
        o_ref[...] = acc + mu_ref[...] * (0.5 * (prev + nxt) - acc)


def _project(x2d, w_packed, mu_packed, seq_len):
    t_rows = x2d.shape[0]
    tm = min(1024, seq_len)
    tiles_per_seq = seq_len // tm
    n_tiles = t_rows // tm
    hb = tm // SUBLANES
    last8 = t_rows // SUBLANES - 1
    kern = functools.partial(_proj_kernel, tiles_per_seq=tiles_per_seq, tm=tm)
    return pl.pallas_call(
        kern,
        grid=(n_tiles, NP_COLS // PROJ_TN),
        in_specs=[
            pl.BlockSpec((tm, D_MODEL), lambda i, j: (i, 0)),
            pl.BlockSpec((SUBLANES, D_MODEL), lambda i, j: (jnp.maximum(i * hb - 1, 0), 0)),
            pl.BlockSpec((SUBLANES, D_MODEL), lambda i, j: (jnp.minimum((i + 1) * hb, last8), 0)),
            pl.BlockSpec((D_MODEL, PROJ_TN), lambda i, j: (0, j)),
            pl.BlockSpec((1, PROJ_TN), lambda i, j: (0, j)),
        ],
        out_specs=pl.BlockSpec((tm, PROJ_TN), lambda i, j: (i, j)),
        out_shape=jax.ShapeDtypeStruct((t_rows, NP_COLS), F32),
        scratch_shapes=[pltpu.VMEM((tm, D_MODEL), BF16), pltpu.VMEM((2 * SUBLANES, D_MODEL), BF16)],
        compiler_params=pltpu.CompilerParams(
            dimension_semantics=("arbitrary", "arbitrary"), vmem_limit_bytes=VMEM_LIMIT_BYTES),
        name="proj_shift",
    )(x2d, x2d, x2d, w_packed, mu_packed)


def _attn_kernel(sink_ref, q_ref, kvp_ref, kvc_ref, kvn_ref, bias_ref, wmask_ref, o_ref, *, blocks_per_seq):
    i = pl.program_id(0)
    pos = i % blocks_per_seq
    blk = ATTN_BLOCK
    kv = jnp.concatenate([kvp_ref[...], kvc_ref[...], kvn_ref[...]], axis=0)
    kidx = lax.broadcasted_iota(jnp.int32, (1, 3 * blk), 1)
    seq_ok = ((kidx >= blk) | (pos > 0)) & ((kidx < 2 * blk) | (pos < blocks_per_seq - 1))
    ok = (wmask_ref[...] > 0.5) & seq_ok
    lane = lax.broadcasted_iota(jnp.int32, (1, PAIR), 1)
    lo = lane < HEAD_DIM
    for pair in range(ATTN_HEADS // 2):
        j = pair // 2
        q2 = (q_ref[:, pair * PAIR:(pair + 1) * PAIR] * (HEAD_DIM ** -0.5)).astype(BF16)
        kd = kv[:, j * PAIR:(j + 1) * PAIR]
        vd = kv[:, KV_HEADS * PAIR + j * PAIR:KV_HEADS * PAIR + (j + 1) * PAIR]
        acc = None
        for e in range(2):
            h = pair * 2 + e
            m_e = lo if e == 0 else jnp.logical_not(lo)
            ke = jnp.where(m_e, kd, 0.0).astype(BF16)
            s = _dot_nt(q2, ke)
            s = jnp.where(ok, s + bias_ref[h], NEG_INF)
            sink = sink_ref[h]
            m = jnp.maximum(jnp.max(s, axis=-1, keepdims=True), sink)
            p = jnp.exp(s - m)
            denom = jnp.sum(p, axis=-1, keepdims=True) + jnp.exp(sink - m)
            ve = jnp.where(m_e, vd, 0.0).astype(BF16)
            o = _dot(p.astype(BF16), ve) / denom
            acc = o if acc is None else acc + o
        o_ref[:, pair * PAIR:(pair + 1) * PAIR] = acc.astype(o_ref.dtype)


def _t5_bucket(rel):
    half = REL_BUCKETS // 2
    max_exact = half // 2
    ret = jnp.where(rel > 0, half, 0)
    n = jnp.abs(rel)
    nf = jnp.maximum(n, 1).astype(F32)
    large = max_exact + (jnp.log(nf / max_exact) / math.log(REL_MAX_DIST / max_exact)
                         * (half - max_exact)).astype(jnp.int32)
    large = jnp.minimum(large, half - 1)
    return ret + jnp.where(n < max_exact, n, large)


def _attention(proj, rel_bias, attn_sink, seq_len):
    t_rows = proj.shape[0]
    blk = ATTN_BLOCK
    nblk = t_rows // blk
    bps = seq_len // blk
    qi = jnp.arange(blk)[:, None]
    kj = jnp.arange(3 * blk)[None, :]
    rel = kj - blk - qi
    bias = jnp.transpose(rel_bias.astype(F32)[_t5_bucket(rel)], (2, 0, 1))
    wmask = (jnp.abs(rel) <= WINDOW).astype(F32)
    kvcol = COL_KDUP // 1024
    kern = functools.partial(_attn_kernel, blocks_per_seq=bps)
    return pl.pallas_call(
        kern,
        grid=(nblk,),
        in_specs=[
            pl.BlockSpec(memory_space=pltpu.SMEM),
            pl.BlockSpec((blk, ATTN_WIDTH), lambda i: (i, 0)),
            pl.BlockSpec((blk, 1024), lambda i: (jnp.maximum(i - 1, 0), kvcol)),
            pl.BlockSpec((blk, 1024), lambda i: (i, kvcol)),
            pl.BlockSpec((blk, 1024), lambda i: (jnp.minimum(i + 1, nblk - 1), kvcol)),
            pl.BlockSpec((ATTN_HEADS, blk, 3 * blk), lambda i: (0, 0, 0)),
            pl.BlockSpec((blk, 3 * blk), lambda i: (0, 0)),
        ],
        out_specs=pl.BlockSpec((blk, ATTN_WIDTH), lambda i: (i, 0)),
        out_shape=jax.ShapeDtypeStruct((t_rows, ATTN_WIDTH), BF16),
        compiler_params=pltpu.CompilerParams(
            dimension_semantics=("arbitrary",), vmem_limit_bytes=VMEM_LIMIT_BYTES),
        name="swa_attention",
    )(attn_sink.astype(F32), proj, proj, proj, proj, bias, wmask)


def _segsum(x, eblk_ref):
    hi, lo = _split2(x)
    outs = []
    for g in range(RWKV_WIDTH // 256):
        sl = slice(g * 256, (g + 1) * 256)
        outs.append(_dot(hi[:, sl], eblk_ref[...]) + _dot(lo[:, sl], eblk_ref[...]))
    return jnp.concatenate(outs, axis=1)


def _rwkv_kernel(r_ref, k_ref, v_ref, lo_ref, w0_ref, w2_ref, a0_ref, a2_ref, kk_ref, ka_ref, rk_ref, eblk_ref,
                 y_ref, state_ref, rs_ref, as_ref, bs_ref, ks_ref, cum_ref, bon_ref, *, rows, chunk):
    d = pl.program_id(0)
    i = pl.program_id(2)
    nch = rows // chunk
    sgn = 1 - 2 * d

    @pl.when(i == 0)
    def _():
        state_ref[...] = jnp.zeros_like(state_ref)

    rr = r_ref[...]
    kx = k_ref[...]
    vx = v_ref[...]
    lora = lo_ref[...]
    wd = jnp.tanh(lora[:, GATE_PAD:GATE_PAD + 2 * DECAY_LORA]).astype(BF16)
    ad = lora[:, GATE_PAD + 2 * DECAY_LORA:LORA_W].astype(BF16)
    wlog = w0_ref[0] + _dot(wd, w2_ref[0])
    z = -wlog
    softplus = jnp.maximum(z, 0.0) + jnp.log(1.0 + jnp.exp(-jnp.abs(z)))
    ld = -jnp.exp(-softplus - 0.5)
    asig = 1.0 / (1.0 + jnp.exp(-(a0_ref[0] + _dot(ad, a2_ref[0]))))
    kd = kx * (1.0 + (asig - 1.0) * ka_ref[...])
    kkr = kx * kk_ref[...]
    n2 = _segsum(kkr * kkr, eblk_ref)
    kk = kkr / jnp.maximum(jnp.sqrt(n2), 1e-12)
    bon_ref[...] = _segsum(rr * kd * rk_ref[...], eblk_ref) * vx

    ri = lax.broadcasted_iota(jnp.int32, (rows, rows), 0)
    ci = lax.broadcasted_iota(jnp.int32, (rows, rows), 1)
    tri = ((ri // chunk) == (ci // chunk)) & (sgn * (ri - ci) >= 0)
    tri = jnp.where(tri, 1.0, 0.0).astype(BF16)
    h3, m3, l3 = _split3(ld)
    cum = _dot(tri, h3) + _dot(tri, m3) + _dot(tri, l3)
    cum_ref[...] = cum
    e_pos = jnp.exp(cum)
    e_neg = jnp.exp(-cum)
    rs_ref[...] = (rr * e_pos).astype(BF16)
    as_ref[...] = (-kk * jnp.exp(cum - ld)).astype(BF16)
    bs_ref[...] = kk * asig * e_neg
    ks_ref[...] = kd * e_neg

    r128 = lax.broadcasted_iota(jnp.int32, (PAIR, PAIR), 0)
    c128 = lax.broadcasted_iota(jnp.int32, (PAIR, PAIR), 1)
    dl = sgn * ((r128 % chunk) - (c128 % chunk))
    diag = (r128 // chunk) == (c128 // chunk)
    strict = dl > 0
    incl = dl >= 0
    m_l = strict & diag
    m_ak = strict & jnp.logical_not(diag)
    m_rb = incl & diag
    m_rk = incl & jnp.logical_not(diag)
    lane_lo = lax.broadcasted_iota(jnp.int32, (chunk, PAIR), 1) < RWKV_HEAD
    row_lo = r128 < chunk

    def chunk_body(cc, carry):
        c = cc + d * (nch - 1 - 2 * cc)
        r0 = pl.multiple_of(c * chunk, chunk)
        trow = r0 + (chunk - 1) * (1 - d)
        etot = jnp.exp(cum_ref[pl.ds(trow, 1), :])
        for p in range(N_PAIRS):
            sl = slice(p * PAIR, (p + 1) * PAIR)
            bp = bs_ref[pl.ds(r0, chunk), sl]
            kp = ks_ref[pl.ds(r0, chunk), sl]
            xt = jnp.concatenate([jnp.where(lane_lo, bp, kp), jnp.where(lane_lo, kp, bp)], axis=0).T
            w_h0 = jnp.where(row_lo, xt, 0.0).astype(BF16)
            w_h1 = jnp.where(row_lo, 0.0, xt).astype(BF16)
            hbd = state_ref[p]
            ar = jnp.concatenate([as_ref[pl.ds(r0, chunk), sl], rs_ref[pl.ds(r0, chunk), sl]], axis=0)
            gc = _dot(ar, jnp.concatenate([w_h0, w_h1, hbd.astype(BF16)], axis=1))
            g0 = gc[:, 0:PAIR]
            g1 = gc[:, PAIR:2 * PAIR]
            arh = gc[:, 2 * PAIR:3 * PAIR]
            ga = jnp.concatenate([g0[:chunk], g1[:chunk]], axis=0)
            gr = jnp.concatenate([g0[chunk:], g1[chunk:]], axis=0)
            l_bd = jnp.where(m_l, ga, 0.0).astype(BF16)
            p_ak = jnp.where(m_ak, ga, 0.0).astype(BF16)
            p_rb = jnp.where(m_rb, gr, 0.0).astype(BF16)
            p_rk = jnp.where(m_rk, gr, 0.0).astype(BF16)
            vp = v_ref[pl.ds(r0, chunk), sl]
            vbig = jnp.concatenate([jnp.where(lane_lo, 0.0, vp), jnp.where(lane_lo, vp, 0.0)], axis=0).astype(BF16)
            arh_a = arh[:chunk]
            arh_r = arh[chunk:]
            u = jnp.concatenate([jnp.where(lane_lo, arh_a, 0.0), jnp.where(lane_lo, 0.0, arh_a)], axis=0)
            u = u + _dot(p_ak, vbig)
            lpow = l_bd
            for step in range(6):
                ub = u.astype(BF16)
                if step < 5:
                    lu = _dot(lpow, jnp.concatenate([lpow, ub], axis=1))
                    u = u + lu[:, PAIR:]
                    lpow = lu[:, :PAIR].astype(BF16)
                else:
                    u = u + _dot(lpow, ub)
            wc = jnp.concatenate([u.astype(BF16), vbig], axis=0)
            xt_d = jnp.where(diag, xt, 0.0).astype(BF16)
            xt_a = jnp.where(diag, 0.0, xt).astype(BF16)
            lhs2 = jnp.concatenate([jnp.concatenate([p_rb, p_rk], axis=1),
                                    jnp.concatenate([xt_d, xt_a], axis=1)], axis=0)
            o2 = _dot(lhs2, wc)
            ybig = jnp.concatenate([jnp.where(lane_lo, arh_r, 0.0), jnp.where(lane_lo, 0.0, arh_r)], axis=0)
            ybig = ybig + o2[:PAIR]
            dcol = jnp.broadcast_to(etot[:, sl], (PAIR, PAIR)).T
            state_ref[p] = dcol * (hbd + o2[PAIR:])
            y_ref[0, pl.ds(r0, chunk), sl] = ybig[:chunk] + ybig[chunk:] + bon_ref[pl.ds(r0, chunk), sl]
        return carry

    lax.fori_loop(0, nch, chunk_body, 0)


def _rwkv_scan(proj, prm, batch, seq_len):
    t_rows = proj.shape[0]
    rows = min(256, seq_len)
    nsteps = seq_len // rows

    def rowblk(d, b, i):
        return b * nsteps + i + d * (nsteps - 1 - 2 * i)

    def colspec(width, col):
        return pl.BlockSpec((rows, width), lambda d, b, i: (rowblk(d, b, i), col))

    def dirspec(shape):
        return pl.BlockSpec((1,) + shape, lambda d, b, i: (d,) + (0,) * len(shape))

    def full(shape):
        return pl.BlockSpec(shape, lambda d, b, i: (0,) * len(shape))

    kern = functools.partial(_rwkv_kernel, rows=rows, chunk=RWKV_CHUNK)
    return pl.pallas_call(
        kern,
        grid=(2, batch, nsteps),
        in_specs=[
            colspec(RWKV_WIDTH, COL_R // RWKV_WIDTH),
            colspec(RWKV_WIDTH, COL_K // RWKV_WIDTH),
            colspec(RWKV_WIDTH, COL_V // RWKV_WIDTH),
            colspec(LORA_W, COL_LORA // LORA_W),
            dirspec((1, RWKV_WIDTH)),
            dirspec((2 * DECAY_LORA, RWKV_WIDTH)),
            dirspec((1, RWKV_WIDTH)),
            dirspec((2 * ICLR_LORA, RWKV_WIDTH)),
            full((1, RWKV_WIDTH)),
            full((1, RWKV_WIDTH)),
            full((1, RWKV_WIDTH)),
            full((256, 256)),
        ],
        out_specs=pl.BlockSpec((1, rows, RWKV_WIDTH), lambda d, b, i: (d, rowblk(d, b, i), 0)),
        out_shape=jax.ShapeDtypeStruct((2, t_rows, RWKV_WIDTH), F32),
        scratch_shapes=[
            pltpu.VMEM((N_PAIRS, PAIR, PAIR), F32),
            pltpu.VMEM((rows, RWKV_WIDTH), BF16),
            pltpu.VMEM((rows, RWKV_WIDTH), BF16),
            pltpu.VMEM((rows, RWKV_WIDTH), F32),
            pltpu.VMEM((rows, RWKV_WIDTH), F32),
            pltpu.VMEM((rows, RWKV_WIDTH), F32),
            pltpu.VMEM((rows, RWKV_WIDTH), F32),
        ],
        compiler_params=pltpu.CompilerParams(
            dimension_semantics=("arbitrary", "arbitrary", "arbitrary"), vmem_limit_bytes=VMEM_LIMIT_BYTES),
        name="rwkv7_scan",
    )(proj, proj, proj, proj, prm["w0"], prm["w2"], prm["a0"], prm["a2"], prm["k_k"], prm["k_a"], prm["r_k"],
      prm["eblk"])


def _rwkv_params(decay_w0, decay_w2, iclr_a0, iclr_a2, k_k, k_a, r_k):
    zw = jnp.zeros_like(decay_w2[0])
    w2 = jnp.stack([jnp.concatenate([decay_w2[0], zw], 0), jnp.concatenate([zw, decay_w2[1]], 0)]).astype(BF16)
    za = jnp.zeros_like(iclr_a2[0])
    a2 = jnp.stack([jnp.concatenate([iclr_a2[0], za], 0), jnp.concatenate([za, iclr_a2[1]], 0)]).astype(BF16)
    blk = np.arange(256) // RWKV_HEAD
    eblk = jnp.asarray((blk[:, None] == blk[None, :]).astype(np.float32), BF16)
    return dict(w0=decay_w0.reshape(2, 1, RWKV_WIDTH), w2=w2, a0=iclr_a0.reshape(2, 1, RWKV_WIDTH), a2=a2,
                k_k=k_k.reshape(1, RWKV_WIDTH), k_a=k_a.reshape(1, RWKV_WIDTH), r_k=r_k.reshape(1, RWKV_WIDTH),
                eblk=eblk)


ROUTE_W = 8
ROUTER_PAD = 128


def _layer_norm_rows(t, g, b):
    mu = jnp.mean(t, axis=-1, keepdims=True)
    tc = t - mu
    var = jnp.mean(tc * tc, axis=-1, keepdims=True)
    return tc * lax.rsqrt(var + LN_EPS) * g + b


def _mix_kernel(yf_ref, yb_ref, attn_ref, gd_ref, x_ref, g2_ref, gng_ref, gnb_ref, wout_ref, lng_ref, lnb_ref,
                wr_ref, br_ref, eblk_ref, h_ref, route_ref):
    y = yf_ref[0] + yb_ref[0]
    inv_n = 1.0 / RWKV_HEAD
    ym = _segsum(y, eblk_ref) * inv_n
    yc = y - ym
    yv = _segsum(yc * yc, eblk_ref) * inv_n
    yn = yc * lax.rsqrt(yv + GN_EPS) * gng_ref[...] + gnb_ref[...]
    sg = 1.0 / (1.0 + jnp.exp(-gd_ref[...]))
    gate = _dot(sg.astype(BF16), g2_ref[...])
    rw = (yn * gate).astype(BF16)
    mix = _dot(attn_ref[...], wout_ref[0:ATTN_WIDTH, :]) + _dot(rw, wout_ref[ATTN_WIDTH:D_MODEL, :])
    h = _layer_norm_rows(ALPHA * x_ref[...] + mix, lng_ref[...], lnb_ref[...])
    h_ref[...] = h

    hh, hl = _split2(h)
    logits = _dot(hh, wr_ref[0]) + _dot(hl, wr_ref[0]) + _dot(hh, wr_ref[1]) + br_ref[...]
    lane = lax.broadcasted_iota(jnp.int32, logits.shape, 1)
    lane_f = lane.astype(F32)
    big = float(ROUTER_PAD)
    neg = -jnp.inf
    is_g = lane < N_GROUPS
    gl = jnp.where(is_g, logits, neg)
    gmax = jnp.max(gl, axis=-1, keepdims=True)
    g_idx = jnp.min(jnp.where(gl == gmax, lane_f, big), axis=-1, keepdims=True).astype(jnp.int32)
    g_gate = 1.0 / jnp.sum(jnp.where(is_g, jnp.exp(logits - gmax), 0.0), axis=-1, keepdims=True)
    e_lane = lane - N_GROUPS
    in_grp = (e_lane >= 0) & (e_lane < N_EXPERTS) & (lax.shift_right_arithmetic(e_lane, 3) == g_idx)
    el = jnp.where(in_grp, logits, neg)
    m1 = jnp.max(el, axis=-1, keepdims=True)
    i1 = jnp.min(jnp.where(el == m1, lane_f, big), axis=-1, keepdims=True)
    el2 = jnp.where(lane_f == i1, neg, el)
    m2 = jnp.max(el2, axis=-1, keepdims=True)
    i2 = jnp.min(jnp.where(el2 == m2, lane_f, big), axis=-1, keepdims=True)
    ex = jnp.exp(m2 - m1)
    w1 = g_gate / (1.0 + ex)
    w2 = g_gate * ex / (1.0 + ex)
    rl = lax.broadcasted_iota(jnp.int32, (logits.shape[0], ROUTE_W), 1)
    rec = jnp.where(rl == 0, i1 - N_GROUPS,
                    jnp.where(rl == 1, i2 - N_GROUPS,
                              jnp.where(rl == 2, w1, jnp.where(rl == 3, w2, 0.0))))
    route_ref[...] = rec


def _mix_route(y2, attn, proj, x2d, prm):
    t_rows = x2d.shape[0]
    tm = min(512, t_rows)
    n_tiles = t_rows // tm

    def full(shape):
        return pl.BlockSpec(shape, lambda i: (0,) * len(shape))

    return pl.pallas_call(
        _mix_kernel,
        grid=(n_tiles,),
        in_specs=[
            pl.BlockSpec((1, tm, RWKV_WIDTH), lambda i: (0, i, 0)),
            pl.BlockSpec((1, tm, RWKV_WIDTH), lambda i: (1, i, 0)),
            pl.BlockSpec((tm, ATTN_WIDTH), lambda i: (i, 0)),
            pl.BlockSpec((tm, GATE_PAD), lambda i: (i, COL_LORA // GATE_PAD)),
            pl.BlockSpec((tm, D_MODEL), lambda i: (i, 0)),
            full((GATE_PAD, RWKV_WIDTH)),
            full((1, RWKV_WIDTH)),
            full((1, RWKV_WIDTH)),
            full((D_MODEL, D_MODEL)),
            full((1, D_MODEL)),
            full((1, D_MODEL)),
            full((2, D_MODEL, ROUTER_PAD)),
            full((1, ROUTER_PAD)),
            full((256, 256)),
        ],
        out_specs=[
            pl.BlockSpec((tm, D_MODEL), lambda i: (i, 0)),
            pl.BlockSpec((tm, ROUTE_W), lambda i: (i, 0)),
        ],
        out_shape=[
            jax.ShapeDtypeStruct((t_rows, D_MODEL), F32),
            jax.ShapeDtypeStruct((t_rows, ROUTE_W), F32),
        ],
        compiler_params=pltpu.CompilerParams(
            dimension_semantics=("arbitrary",), vmem_limit_bytes=VMEM_LIMIT_BYTES),
        name="mix_ln_route",
    )(y2, y2, attn, proj, x2d, prm["g2"], prm["gn_g"], prm["gn_b"], prm["w_out"], prm["ln1_g"], prm["ln1_b"],
      prm["wr"], prm["br"], prm["eblk"])


MOE_ROWS = 256


def _moe_kernel(blk_e_ref, nblk_ref, tok_ref, h_hbm, wg_ref, wu_ref, wd_ref, o_ref,
                xbuf, wgb, wub, wdb, sem):
    i = pl.program_id(0)

    def row_copy(r):
        return pltpu.make_async_copy(h_hbm.at[tok_ref[0, 0, r]], xbuf.at[r], sem)

    @pl.when(i < nblk_ref[0])
    def _():
        def issue(r, c):
            row_copy(r).start()
            return c
        lax.fori_loop(0, MOE_ROWS, issue, 0, unroll=8)

        e = blk_e_ref[i]
        prev = blk_e_ref[jnp.maximum(i - 1, 0)]

        @pl.when((i == 0) | (e != prev))
        def _():
            wgb[...] = wg_ref[0].astype(BF16)
            wub[...] = wu_ref[0].astype(BF16)
            wdb[...] = wd_ref[0].astype(BF16)

        pltpu.make_async_copy(h_hbm.at[pl.ds(0, MOE_ROWS)], xbuf, sem).wait()
        xb = xbuf[...].astype(BF16)
        gt = _dot(xb, wgb[...])
        up = _dot(xb, wub[...])
        hid = (gt / (1.0 + jnp.exp(-gt)) * up).astype(BF16)
        o_ref[...] = _dot(hid, wdb[...])

    @pl.when(i >= nblk_ref[0])
    def _():
        o_ref[...] = jnp.zeros_like(o_ref)


def _moe_experts(h, row_tok, blk_e, nblk, w_gate, w_up, w_down):
    n_blocks = row_tok.shape[0]

    def wspec(shape):
        return pl.BlockSpec((1,) + shape, lambda i, be, nb: (be[i], 0, 0))

    grid_spec = pltpu.PrefetchScalarGridSpec(
        num_scalar_prefetch=2,
        grid=(n_blocks,),
        in_specs=[
            pl.BlockSpec((1, 1, MOE_ROWS), lambda i, be, nb: (i, 0, 0), memory_space=pltpu.SMEM),
            pl.BlockSpec(memory_space=pl.ANY),
            wspec((D_MODEL, D_EXPERT)),
            wspec((D_MODEL, D_EXPERT)),
            wspec((D_EXPERT, D_MODEL)),
        ],
        out_specs=pl.BlockSpec((MOE_ROWS, D_MODEL), lambda i, be, nb: (i, 0)),
        scratch_shapes=[
            pltpu.VMEM((MOE_ROWS, D_MODEL), F32),
            pltpu.VMEM((D_MODEL, D_EXPERT), BF16),
            pltpu.VMEM((D_MODEL, D_EXPERT), BF16),
            pltpu.VMEM((D_EXPERT, D_MODEL), BF16),
            pltpu.SemaphoreType.DMA(()),
        ],
    )
    return pl.pallas_call(
        _moe_kernel,
        grid_spec=grid_spec,
        out_shape=jax.ShapeDtypeStruct((n_blocks * MOE_ROWS, D_MODEL), F32),
        compiler_params=pltpu.CompilerParams(
            dimension_semantics=("arbitrary",), vmem_limit_bytes=VMEM_LIMIT_BYTES),
        name="moe_experts",
    )(blk_e, nblk, row_tok, h, w_gate, w_up, w_down)


def _final_kernel(pos_ref, h_ref, route_ref, yb_hbm, g_ref, b_ref, o_ref, buf, sem, *, tm):
    def row_copy(r, k):
        return pltpu.make_async_copy(yb_hbm.at[pos_ref[0, 0, 2 * r + k]], buf.at[k, r], sem)

    def issue(r, c):
        row_copy(r, 0).start()
        row_copy(r, 1).start()
        return c
    lax.fori_loop(0, tm, issue, 0, unroll=8)
    for k in range(2):
        pltpu.make_async_copy(yb_hbm.at[pl.ds(0, tm)], buf.at[k], sem).wait()
    route = route_ref[...]
    ffn = buf[0] * route[:, 2:3] + buf[1] * route[:, 3:4]
    o_ref[...] = _layer_norm_rows(ALPHA * h_ref[...] + ffn, g_ref[...], b_ref[...])


def _combine_ln(pos, h, route, yb, ln_g, ln_b):
    t_rows = h.shape[0]
    tm = pos.shape[2] // 2
    kern = functools.partial(_final_kernel, tm=tm)
    return pl.pallas_call(
        kern,
        grid=(t_rows // tm,),
        in_specs=[
            pl.BlockSpec((1, 1, 2 * tm), lambda i: (i, 0, 0), memory_space=pltpu.SMEM),
            pl.BlockSpec((tm, D_MODEL), lambda i: (i, 0)),
            pl.BlockSpec((tm, ROUTE_W), lambda i: (i, 0)),
            pl.BlockSpec(memory_space=pl.ANY),
            pl.BlockSpec((1, D_MODEL), lambda i: (0, 0)),
            pl.BlockSpec((1, D_MODEL), lambda i: (0, 0)),
        ],
        out_specs=pl.BlockSpec((tm, D_MODEL), lambda i: (i, 0)),
        out_shape=jax.ShapeDtypeStruct((t_rows, D_MODEL), F32),
        scratch_shapes=[pltpu.VMEM((2, tm, D_MODEL), F32), pltpu.SemaphoreType.DMA(())],
        compiler_params=pltpu.CompilerParams(
            dimension_semantics=("arbitrary",), vmem_limit_bytes=VMEM_LIMIT_BYTES),
        name="combine_ln",
    )(pos, h, route, yb, ln_g, ln_b)


FINAL_TM = 256


def _dispatch_tables(route):
    t_rows = route.shape[0]
    n = 2 * t_rows
    flat_e = route[:, 0:2].astype(jnp.int32).reshape(n)
    order = jnp.argsort(flat_e)
    sorted_e = flat_e[order]
    counts = jnp.bincount(flat_e, length=N_EXPERTS).astype(jnp.int32)
    start = jnp.cumsum(counts) - counts
    padded = (counts + MOE_ROWS - 1) // MOE_ROWS * MOE_ROWS
    pad_end = jnp.cumsum(padded)
    pad_start = pad_end - padded
    dest = pad_start[sorted_e] + jnp.arange(n, dtype=jnp.int32) - start[sorted_e]
    n_blocks = n // MOE_ROWS + N_EXPERTS
    r_rows = n_blocks * MOE_ROWS
    row_tok = jnp.zeros((r_rows,), jnp.int32).at[dest].set((order // 2).astype(jnp.int32))
    pos = jnp.zeros((n,), jnp.int32).at[order].set(dest)
    blk_start = jnp.arange(n_blocks, dtype=jnp.int32) * MOE_ROWS
    blk_e = jnp.minimum(jnp.searchsorted(pad_end, blk_start, side='right'), N_EXPERTS - 1).astype(jnp.int32)
    nblk = (pad_end[-1] // MOE_ROWS).astype(jnp.int32).reshape(1)
    return (row_tok.reshape(n_blocks, 1, MOE_ROWS), blk_e, nblk,
            pos.reshape(t_rows // FINAL_TM, 1, 2 * FINAL_TM))


def _trunk(x, prm):
    batch, seq_len, _ = x.shape
    x2d = x.reshape(batch * seq_len, D_MODEL)
    proj = _project(x2d, prm["w_packed"], prm["mu_packed"], seq_len)
    attn = _attention(proj, prm["rel_bias"], prm["attn_sink"], seq_len)
    y2 = _rwkv_scan(proj, prm, batch, seq_len)
    h, route = _mix_route(y2, attn, proj, x2d, prm)
    row_tok, blk_e, nblk, pos = _dispatch_tables(route)
    yb = _moe_experts(h, row_tok, blk_e, nblk, prm["w_gate"], prm["w_up"], prm["w_down"])
    out = _combine_ln(pos, h, route, yb, prm["ln2_g"], prm["ln2_b"])
    return out.reshape(batch, seq_len, D_MODEL)


def _prepare_params(rel_bias, w_in, attn_sink, shift_mu, decay_w0, decay_w2, iclr_a0, iclr_a2, gate_g2,
                    k_k, k_a, r_k, gn_g, gn_b, w_out, ln1_g, ln1_b, router_group_w, router_group_b,
                    router_expert_w, router_expert_b, w_gate, w_up, w_down, ln2_g, ln2_b):
    idx = _packed_column_index()
    keep = jnp.asarray(idx >= 0)
    src = np.maximum(idx, 0)
    w_packed = jnp.where(keep[None, :], w_in[:, src], 0.0).astype(BF16)
    n_attn = ATTN_WIDTH + 2 * KV_WIDTH
    mu_full = jnp.concatenate([jnp.zeros((n_attn,), F32), shift_mu])
    mu_packed = jnp.where(keep, mu_full[src], 0.0).reshape(1, NP_COLS)
    prm = _rwkv_params(decay_w0, decay_w2, iclr_a0, iclr_a2, k_k, k_a, r_k)
    g2 = jnp.concatenate([gate_g2, jnp.zeros((GATE_PAD - GATE_LORA, RWKV_WIDTH), F32)], 0).astype(BF16)
    wr = jnp.concatenate([router_group_w, router_expert_w,
                          jnp.zeros((D_MODEL, ROUTER_PAD - N_GROUPS - N_EXPERTS), F32)], axis=1)
    wr_hi = wr.astype(BF16)
    wr_lo = (wr - wr_hi.astype(F32)).astype(BF16)
    br = jnp.concatenate([router_group_b, router_expert_b,
                          jnp.zeros((ROUTER_PAD - N_GROUPS - N_EXPERTS,), F32)]).reshape(1, ROUTER_PAD)
    prm.update(
        w_packed=w_packed, mu_packed=mu_packed, rel_bias=rel_bias, attn_sink=attn_sink, g2=g2,
        gn_g=gn_g.reshape(1, RWKV_WIDTH), gn_b=gn_b.reshape(1, RWKV_WIDTH), w_out=w_out.astype(BF16),
        ln1_g=ln1_g.reshape(1, D_MODEL), ln1_b=ln1_b.reshape(1, D_MODEL), wr=jnp.stack([wr_hi, wr_lo]), br=br,
        w_gate=w_gate, w_up=w_up, w_down=w_down, ln2_g=ln2_g.reshape(1, D_MODEL), ln2_b=ln2_b.reshape(1, D_MODEL))
    return prm


def kernel(x_prompt, x_sample, rel_bias, w_in, attn_sink, shift_mu, decay_w0, decay_w2, iclr_a0, iclr_a2,
           gate_g2, k_k, k_a, r_k, gn_g, gn_b, w_out, ln1_g, ln1_b, router_group_w, router_group_b,
           router_expert_w, router_expert_b, w_gate, w_up, w_down, ln2_g, ln2_b):
    layer = 0
    prm = _prepare_params(rel_bias, w_in[layer], attn_sink[layer], shift_mu[layer], decay_w0[layer],
                          decay_w2[layer], iclr_a0[layer], iclr_a2[layer], gate_g2[layer], k_k[layer],
                          k_a[layer], r_k[layer], gn_g[layer], gn_b[layer], w_out[layer], ln1_g[layer],
                          ln1_b[layer], router_group_w[layer], router_group_b[layer], router_expert_w[layer],
                          router_expert_b[layer], w_gate[layer], w_up[layer], w_down[layer], ln2_g[layer],
                          ln2_b[layer])
    return (_trunk(x_prompt, prm), _trunk(x_sample, prm))
```
